```python
import jax, jax.numpy as jnp
from jax import lax
import numpy as np

D_MODEL = 1024
BATCH = 16
SEQ = 256
DEPTH = 2
DEC_BATCH = 4
DEC_SEQ = 2048
PAST_LEN = 256

GRID_W = 64
N_EVEN = (DEPTH + 1) // 2
N_ODD = DEPTH // 2
HEAD_DIM = 64
N_HEADS = 8
N_KV_HEADS = 2
GROUP = N_HEADS // N_KV_HEADS
ATT_WIDTH = N_HEADS * HEAD_DIM
KV_WIDTH = N_KV_HEADS * HEAD_DIM
WINDOW = 128
BLK = 128
ROPE_BASE = 10000.0
CONV_CH = 512
CONV_K = 31
IN_EVEN = ATT_WIDTH + 2 * KV_WIDTH + 2 * CONV_CH
MIX_EVEN = ATT_WIDTH + CONV_CH
LRU_WIDTH = 1024
LRU_BLOCKS = 8
LRU_BW = LRU_WIDTH // LRU_BLOCKS
LRU_CONV_K = 4
LRU_C = 8.0
D_FF = 4 * D_MODEL
EPS = 1e-6
NEG_INF = -1e30

kernel_name = "hybrid_dit_swa_conformer_rglru_step"


def _rmsnorm(x, g):
    xf = x.astype(jnp.float32)
    y = xf * lax.rsqrt(jnp.mean(xf * xf, axis=-1, keepdims=True) + EPS)
    return (y * g.astype(jnp.float32)).astype(x.dtype)


def _layernorm(x, g, b):
    xf = x.astype(jnp.float32)
    mu = jnp.mean(xf, axis=-1, keepdims=True)
    var = jnp.mean(jnp.square(xf - mu), axis=-1, keepdims=True)
    y = (xf - mu) * lax.rsqrt(var + EPS)
    return (y * g.astype(jnp.float32) + b.astype(jnp.float32)).astype(x.dtype)


def _adaln(cond, w_mod, b_mod):
    return jnp.split(jax.nn.silu(cond) @ w_mod + b_mod, 6, axis=-1)


def _modulate(x, g, shift, scale):
    return _rmsnorm(x, g) * (1 + scale) + shift


def _sq_relu_mlp(h, w1, w2):
    return jnp.square(jax.nn.relu(h @ w1)) @ w2


def _dwconv(x, w, b, pad_l, pad_r):
    y = lax.conv_general_dilated(
        x, w[:, None, :].astype(x.dtype), window_strides=(1,),
        padding=[(pad_l, pad_r)], dimension_numbers=('NWC', 'WIO', 'NWC'),
        feature_group_count=x.shape[-1])
    return y + b


def _rope_axis(x, pos):
    n = x.shape[-1] // 2
    inv = ROPE_BASE ** (-jnp.arange(n, dtype=jnp.float32) / n)
    ang = pos.astype(jnp.float32)[:, None] * inv[None, :]
    cos = jnp.cos(ang)[:, None, :]
    sin = jnp.sin(ang)[:, None, :]
    x1, x2 = x[..., :n], x[..., n:]
    return jnp.concatenate([x1 * cos - x2 * sin, x2 * cos + x1 * sin], axis=-1)


def _axial_rope(x):
    T = x.shape[1]
    rows = T // GRID_W
    row = jnp.repeat(jnp.arange(rows), GRID_W)
    col = jnp.tile(jnp.arange(GRID_W), rows)
    xf = x.astype(jnp.float32)
    h = HEAD_DIM // 2
    out = jnp.concatenate([_rope_axis(xf[..., :h], row), _rope_axis(xf[..., h:], col)], axis=-1)
    return out.astype(x.dtype)


def _attend(q, k, v, mask, sink):
    B, Q = q.shape[0], q.shape[1]
    s = jnp.einsum('bqkgd,bskd->bkgqs', q, k).astype(jnp.float32) * (HEAD_DIM ** -0.5)
    if mask is not None:
        s = jnp.where(mask, s, NEG_INF)
    sink_col = jnp.broadcast_to(sink.astype(jnp.float32)[None, :, :, None, None],
                                (B, N_KV_HEADS, GROUP, Q, 1))
    p = jax.nn.softmax(jnp.concatenate([s, sink_col], axis=-1), axis=-1)[..., :-1]
    return jnp.einsum('bkgqs,bskd->bqkgd', p.astype(v.dtype), v)


def _context_attention(q, k, v, sink):
    B, C = q.shape[0], q.shape[1]
    nq = C // BLK
    qb = q.reshape(B, nq, BLK, N_KV_HEADS, GROUP, HEAD_DIM).transpose(1, 0, 2, 3, 4, 5)
    o = lax.map(lambda qi: _attend(qi, k, v, None, sink), qb)
    return o.transpose(1, 0, 2, 3, 4, 5).reshape(B, C, ATT_WIDTH)


def _latent_attention(q, k, v, ck, cv, sink):
    B, T = q.shape[0], q.shape[1]
    nb = T // BLK
    C = ck.shape[1]
    qb = q.reshape(B, nb, BLK, N_KV_HEADS, GROUP, HEAD_DIM).transpose(1, 0, 2, 3, 4, 5)
    kp = jnp.pad(k, ((0, 0), (BLK, BLK), (0, 0), (0, 0)))
    vp = jnp.pad(v, ((0, 0), (BLK, BLK), (0, 0), (0, 0)))
    a_idx = jnp.arange(BLK)[:, None]
    b_idx = jnp.arange(3 * BLK)[None, :]
    ctx_mask = jnp.ones((BLK, C), dtype=bool)

    def block(args):
        n, qi = args
        kw = lax.dynamic_slice_in_dim(kp, n * BLK, 3 * BLK, axis=1)
        vw = lax.dynamic_slice_in_dim(vp, n * BLK, 3 * BLK, axis=1)
        i = n * BLK + a_idx
        j = n * BLK - BLK + b_idx
        m = (jnp.abs(i - j) <= WINDOW) & (j >= 0) & (j < T)
        return _attend(qi, jnp.concatenate([kw, ck], axis=1), jnp.concatenate([vw, cv], axis=1),
                       jnp.concatenate([m, ctx_mask], axis=1), sink)

    o = lax.map(block, (jnp.arange(nb), qb))
    return o.transpose(1, 0, 2, 3, 4, 5).reshape(B, T, ATT_WIDTH)


def _conformer_conv(u, w, b, g, beta):
    a, gate = jnp.split(u, 2, axis=-1)
    z = _dwconv(a * jax.nn.sigmoid(gate), w, b, CONV_K // 2, CONV_K // 2)
    return jax.nn.silu(_layernorm(z, g, beta))


def _even_split(h, w_in):
    B, T = h.shape[0], h.shape[1]
    q, k, v, u = jnp.split(h @ w_in, [ATT_WIDTH, ATT_WIDTH + KV_WIDTH, ATT_WIDTH + 2 * KV_WIDTH], axis=-1)
    return (q.reshape(B, T, N_HEADS, HEAD_DIM), k.reshape(B, T, N_KV_HEADS, HEAD_DIM),
            v.reshape(B, T, N_KV_HEADS, HEAD_DIM), u)


def _even_ctx(h, w_in, w_out, sink, cw, cb, cg, cbeta):
    q, k, v, u = _even_split(h, w_in)
    o_att = _context_attention(q, k, v, sink.reshape(N_KV_HEADS, GROUP))
    o_conv = _conformer_conv(u, cw, cb, cg, cbeta)
    return jnp.concatenate([o_att, o_conv], axis=-1) @ w_out, k, v


def _even_lat(h, ck, cv, w_in, w_out, sink, cw, cb, cg, cbeta):
    q, k, v, u = _even_split(h, w_in)
    q = _axial_rope(q)
    k = _axial_rope(k)
    o_att = _latent_attention(q, k, v, ck, cv, sink.reshape(N_KV_HEADS, GROUP))
    o_conv = _conformer_conv(u, cw, cb, cg, cbeta)
    return jnp.concatenate([o_att, o_conv], axis=-1) @ w_out


def _lru_coeffs(xc, wa, ba, wx, bx, lam):
    B, T = xc.shape[0], xc.shape[1]
    xb = xc.reshape(B, T, LRU_BLOCKS, LRU_BW)
    r = jax.nn.sigmoid(jnp.einsum('btnc,ncd->btnd', xb, wa.astype(jnp.float32)).reshape(B, T, LRU_WIDTH)
                       + ba.astype(jnp.float32))
    i = jax.nn.sigmoid(jnp.einsum('btnc,ncd->btnd', xb, wx.astype(jnp.float32)).reshape(B, T, LRU_WIDTH)
                       + bx.astype(jnp.float32))
    log_a = -LRU_C * r * jax.nn.softplus(-lam.astype(jnp.float32))
    a = jnp.exp(log_a)
    return a, jnp.sqrt(-jnp.expm1(2.0 * log_a)) * (i * xc)


def _linear_scan(a, b, h0, reverse):
    idx = -1 if reverse else 0
    b = b.at[:, idx].add(a[:, idx] * h0)

    def comb(l, r):
        al, bl = l
        ar, br = r
        return al * ar, ar * bl + br

    _, h = lax.associative_scan(comb, (a, b), reverse=reverse, axis=1)
    return h


def _odd_core(h, h0, w_in, w_out, cw, cb, wa, ba, wx, bx, lam):
    gate_br, rec = jnp.split(h @ w_in, 2, axis=-1)
    xc = _dwconv(rec, cw, cb, 1, 2).astype(jnp.float32)
    a_f, b_f = _lru_coeffs(xc, wa[0], ba[0], wx[0], bx[0], lam[0])
    h_f = _linear_scan(a_f, b_f, h0[:, 0].astype(jnp.float32), False)
    a_b, b_b = _lru_coeffs(xc, wa[1], ba[1], wx[1], bx[1], lam[1])
    h_b = _linear_scan(a_b, b_b, h0[:, 1].astype(jnp.float32), True)
    y = (h_f + h_b).astype(h.dtype) * jax.nn.gelu(gate_br)
    return y @ w_out, h_f, h_b


def setup_inputs(seed: int = 0) -> dict:
    key = jax.random.key(seed)
    ks = jax.random.split(key, 32)

    def nrm(k, shape, scale):
        return jax.random.normal(k, shape, jnp.float32) * scale

    u = jax.random.uniform(ks[28], (N_ODD, 2, LRU_WIDTH), jnp.float32, minval=0.9, maxval=0.999)
    a_init = u ** (1.0 / LRU_C)
    return {
        "x_prompt": nrm(ks[0], (BATCH, SEQ, D_MODEL), 1.0),
        "x_sample": nrm(ks[1], (DEC_BATCH, DEC_SEQ, D_MODEL), 1.0),
        "c": nrm(ks[2], (DEC_BATCH, D_MODEL), 1.0),
        "cache_k": nrm(ks[3], (DEC_BATCH, N_EVEN, PAST_LEN, N_KV_HEADS, HEAD_DIM), 1.0),
        "cache_v": nrm(ks[4], (DEC_BATCH, N_EVEN, PAST_LEN, N_KV_HEADS, HEAD_DIM), 1.0),
        "state_lru": nrm(ks[5], (DEC_BATCH, N_ODD, 2, LRU_WIDTH), 0.5),
        "c_ctx": nrm(ks[6], (D_MODEL,), 1.0),
        "w_mod": nrm(ks[7], (DEPTH, D_MODEL, 6 * D_MODEL), D_MODEL ** -0.5),
        "b_mod": nrm(ks[8], (DEPTH, 6 * D_MODEL), 0.02),
        "norm_mix": 1.0 + nrm(ks[9], (DEPTH, D_MODEL), 0.1),
        "norm_ffn": 1.0 + nrm(ks[10], (DEPTH, D_MODEL), 0.1),
        "w_ff1": nrm(ks[11], (DEPTH, D_MODEL, D_FF), D_MODEL ** -0.5),
        "w_ff2": nrm(ks[12], (DEPTH, D_FF, D_MODEL), D_FF ** -0.5),
        "att_in": nrm(ks[13], (N_EVEN, D_MODEL, IN_EVEN), D_MODEL ** -0.5),
        "att_out": nrm(ks[14], (N_EVEN, MIX_EVEN, D_MODEL), MIX_EVEN ** -0.5),
        "att_sink": nrm(ks[15], (N_EVEN, N_HEADS), 0.5),
        "conv_w": nrm(ks[16], (N_EVEN, CONV_K, CONV_CH), CONV_K ** -0.5),
        "conv_b": nrm(ks[17], (N_EVEN, CONV_CH), 0.02),
        "conv_norm_g": 1.0 + nrm(ks[18], (N_EVEN, CONV_CH), 0.1),
        "conv_norm_b": nrm(ks[19], (N_EVEN, CONV_CH), 0.02),
        "lru_in": nrm(ks[20], (N_ODD, D_MODEL, 2 * LRU_WIDTH), D_MODEL ** -0.5),
        "lru_out": nrm(ks[21], (N_ODD, LRU_WIDTH, D_MODEL), LRU_WIDTH ** -0.5),
        "lru_conv_w": nrm(ks[22], (N_ODD, LRU_CONV_K, LRU_WIDTH), LRU_CONV_K ** -0.5),
        "lru_conv_b": nrm(ks[23], (N_ODD, LRU_WIDTH), 0.02),
        "lru_wa": nrm(ks[24], (N_ODD, 2, LRU_BLOCKS, LRU_BW, LRU_BW), LRU_BW ** -0.5),
        "lru_ba": nrm(ks[25], (N_ODD, 2, LRU_WIDTH), 0.02),
        "lru_wx": nrm(ks[26], (N_ODD, 2, LRU_BLOCKS, LRU_BW, LRU_BW), LRU_BW ** -0.5),
        "lru_bx": nrm(ks[27], (N_ODD, 2, LRU_WIDTH), 0.02),
        "lru_lam": jnp.log(a_init) - jnp.log1p(-a_init),
        "final_norm": 1.0 + nrm(ks[29], (D_MODEL,), 0.1),
    }


def reference(x_prompt, x_sample, c, cache_k, cache_v, state_lru, c_ctx,
              w_mod, b_mod, norm_mix, norm_ffn, w_ff1, w_ff2,
              att_in, att_out, att_sink, conv_w, conv_b, conv_norm_g, conv_norm_b,
              lru_in, lru_out, lru_conv_w, lru_conv_b, lru_wa, lru_ba, lru_wx, lru_bx, lru_lam,
              final_norm):
    y_p = x_prompt
    y_s = x_sample
    cond_ctx = c_ctx[None, None, :]
    cond_lat = c[:, None, :]
    new_k, new_v, new_h = [], [], []
    for li in range(DEPTH):
        mp = _adaln(cond_ctx, w_mod[li], b_mod[li])
        ms = _adaln(cond_lat, w_mod[li], b_mod[li])
        hp = _modulate(y_p, norm_mix[li], mp[0], mp[1])
        hs = _modulate(y_s, norm_mix[li], ms[0], ms[1])
        if li % 2 == 0:
            e = li // 2
            ep = (att_in[e], att_out[e], att_sink[e], conv_w[e], conv_b[e], conv_norm_g[e], conv_norm_b[e])
            op, kp, vp = _even_ctx(hp, *ep)
            new_k.append(kp)
            new_v.append(vp)
            os_ = _even_lat(hs, cache_k[:, e], cache_v[:, e], *ep)
        else:
            o = li // 2
            opar = (lru_in[o], lru_out[o], lru_conv_w[o], lru_conv_b[o],
                    lru_wa[o], lru_ba[o], lru_wx[o], lru_bx[o], lru_lam[o])
            h0 = jnp.zeros((y_p.shape[0], 2, LRU_WIDTH), jnp.float32)
            op, hf_ctx, hb_ctx = _odd_core(hp, h0, *opar)
            new_h.append(jnp.stack([hf_ctx[:, -1], hb_ctx[:, 0]], axis=1).astype(y_p.dtype))
            os_, _, _ = _odd_core(hs, state_lru[:, o], *opar)
        y_p = y_p + mp[2] * op
        y_s = y_s + ms[2] * os_
        y_p = y_p + mp[5] * _sq_relu_mlp(_modulate(y_p, norm_ffn[li], mp[3], mp[4]), w_ff1[li], w_ff2[li])
        y_s = y_s + ms[5] * _sq_relu_mlp(_modulate(y_s, norm_ffn[li], ms[3], ms[4]), w_ff1[li], w_ff2[li])
    y_prompt = _rmsnorm(y_p, final_norm)
    y_sample = _rmsnorm(y_s, final_norm)
    new_cache_k = jnp.stack(new_k, axis=1)
    new_cache_v = jnp.stack(new_v, axis=1)
    new_state_lru = jnp.stack(new_h, axis=1)
    return (y_prompt, y_sample, new_cache_k, new_cache_v, new_state_lru)
```

```python
import functools
import math

import jax
import jax.numpy as jnp
from jax import lax
from jax.experimental import pallas as pl
from jax.experimental.pallas import tpu as pltpu

F32 = jnp.float32
BF16 = jnp.bfloat16

D_MODEL = 1024
N_HEADS = 8
N_KV_HEADS = 2
HEAD_DIM = 64
ATT_WIDTH = N_HEADS * HEAD_DIM
KV_WIDTH = N_KV_HEADS * HEAD_DIM
WINDOW = 128
GRID_W = 64
ROPE_BASE = 10000.0
CONV_CH = 512
CONV_K = 31
LRU_WIDTH = 1024
LRU_BLOCKS = 8
LRU_BW = LRU_WIDTH // LRU_BLOCKS
LRU_CONV_K = 4
LRU_C = 8.0
D_FF = 4 * D_MODEL
EPS = 1e-6
NEG_INF = -1e30

LANES = 128
SUBLANES = 8
BF16_SUBLANES = 16
VMEM_LIMIT_BYTES = 56 * 1024 * 1024

MOD_ROWS = 8
LRU_SEQS = 4
ROW_TILE = 256
MLP_ROW_TILE = 512
FF_CHUNK = 1024
ATT_BQ = 128
CONV_TC = 128
CONV_HALO = 16
LRU_ROWS = 512
LRU_HALO = 16


def _cparams(n_axes):
    return pltpu.CompilerParams(dimension_semantics=("arbitrary",) * n_axes,
                                vmem_limit_bytes=VMEM_LIMIT_BYTES)


def _full(shape):
    n = len(shape)
    return pl.BlockSpec(shape, lambda *_: (0,) * n)


def _norm_modulate(x, g, shift, scale):
    ms = jnp.mean(x * x, axis=-1, keepdims=True)
    return (x * lax.rsqrt(ms + EPS) * g) * (1.0 + scale) + shift


def _sigmoid(x):
    return 1.0 / (1.0 + jnp.exp(-x))


def _mod_kernel(cond_ref, w_ref, b_ref, o_ref):
    c = cond_ref[...]
    s = (c * _sigmoid(c)).astype(BF16)
    o_ref[...] = jnp.dot(s, w_ref[...].astype(BF16), preferred_element_type=F32) + b_ref[...]


def _modulation(cond, w_mod, b_mod):
    depth, _, n6 = w_mod.shape
    tn = n6 // 4
    out = pl.pallas_call(
        _mod_kernel,
        grid=(depth, n6 // tn),
        in_specs=[_full((MOD_ROWS, D_MODEL)),
                  pl.BlockSpec((None, D_MODEL, tn), lambda l, j: (l, 0, j)),
                  pl.BlockSpec((None, 1, tn), lambda l, j: (l, 0, j))],
        out_specs=pl.BlockSpec((None, MOD_ROWS, tn), lambda l, j: (l, 0, j)),
        out_shape=jax.ShapeDtypeStruct((depth, MOD_ROWS, n6), F32),
        compiler_params=_cparams(2),
        name="modulation",
    )(cond, w_mod, b_mod.reshape(depth, 1, n6))
    return out.reshape(depth, MOD_ROWS, 6, D_MODEL)


def _mod_spec(layer, row_of_tile):
    return pl.BlockSpec((None, None, 6, D_MODEL), lambda i: (layer, row_of_tile(i), 0, 0))


def _dup_halves(x, lane_lo):
    sw = pltpu.roll(x, HEAD_DIM, axis=1)
    return jnp.where(lane_lo, x, sw), jnp.where(lane_lo, sw, x)


def _rope_slab(x, cos, sin_signed, first16):
    partner = jnp.where(first16, pltpu.roll(x, LANES - 16, axis=1), pltpu.roll(x, 16, axis=1))
    return x * cos + partner * sin_signed


def _even_in_kernel(rope, *refs):
    if rope:
        (x_ref, mod_ref, g_ref, w_ref, cos_ref, sin_ref,
         q_ref, kd_ref, vd_ref, u_ref) = refs
    else:
        (x_ref, mod_ref, g_ref, w_ref,
         q_ref, kd_ref, vd_ref, u_ref, k_ref, v_ref) = refs
    h = _norm_modulate(x_ref[...], g_ref[...], mod_ref[0:1, :], mod_ref[1:2, :]).astype(BF16)
    tm = h.shape[0]
    lane = lax.broadcasted_iota(jnp.int32, (tm, LANES), 1)
    lane_lo = lane < HEAD_DIM
    q = jnp.dot(h, w_ref[:, 0:ATT_WIDTH], preferred_element_type=F32)
    kv = jnp.dot(h, w_ref[:, ATT_WIDTH:ATT_WIDTH + 2 * KV_WIDTH], preferred_element_type=F32)
    k = kv[:, 0:KV_WIDTH]
    v = kv[:, KV_WIDTH:2 * KV_WIDTH]
    if rope:
        cos = cos_ref[...]
        sin = sin_ref[...]
        first16 = (lane % 32) < 16
        k = _rope_slab(k, cos, sin, first16)
        for j in range(ATT_WIDTH // LANES):
            qs = _rope_slab(q[:, j * LANES:(j + 1) * LANES], cos, sin, first16)
            q_ref[:, j * LANES:(j + 1) * LANES] = qs.astype(BF16)
    else:
        q_ref[...] = q.astype(BF16)
        k_ref[...] = k
        v_ref[...] = v
    k0, k1 = _dup_halves(k, lane_lo)
    v0, v1 = _dup_halves(v, lane_lo)
    kd_ref[:, 0:LANES] = k0.astype(BF16)
    kd_ref[:, LANES:2 * LANES] = k1.astype(BF16)
    vd_ref[:, 0:LANES] = v0.astype(BF16)
    vd_ref[:, LANES:2 * LANES] = v1.astype(BF16)
    u0 = ATT_WIDTH + 2 * KV_WIDTH
    ua = jnp.dot(h, w_ref[:, u0:u0 + CONV_CH], preferred_element_type=F32)
    ug = jnp.dot(h, w_ref[:, u0 + CONV_CH:u0 + 2 * CONV_CH], preferred_element_type=F32)
    u_ref[...] = ua * _sigmoid(ug)


def _even_in(x, mod, layer, row_of_tile, g, w_in, rope_tabs):
    n = x.shape[0]
    tm = ROW_TILE
    rope = rope_tabs is not None
    row = lambda i: (i, 0)
    in_specs = [pl.BlockSpec((tm, D_MODEL), row), _mod_spec(layer, row_of_tile),
                _full((1, D_MODEL)), _full(w_in.shape)]
    args = [x, mod, g, w_in]
    out_specs = [pl.BlockSpec((tm, ATT_WIDTH), row), pl.BlockSpec((tm, 2 * LANES), row),
                 pl.BlockSpec((tm, 2 * LANES), row), pl.BlockSpec((tm, CONV_CH), row)]
    out_shape = [jax.ShapeDtypeStruct((n, ATT_WIDTH), BF16), jax.ShapeDtypeStruct((n, 2 * LANES), BF16),
                 jax.ShapeDtypeStruct((n, 2 * LANES), BF16), jax.ShapeDtypeStruct((n, CONV_CH), F32)]
    if rope:
        cos, sin = rope_tabs
        tiles_per_seq = cos.shape[0] // tm
        tab = lambda i: (i % tiles_per_seq, 0)
        in_specs += [pl.BlockSpec((tm, LANES), tab), pl.BlockSpec((tm, LANES), tab)]
        args += [cos, sin]
    else:
        out_specs += [pl.BlockSpec((tm, KV_WIDTH), row), pl.BlockSpec((tm, KV_WIDTH), row)]
        out_shape += [jax.ShapeDtypeStruct((n, KV_WIDTH), F32), jax.ShapeDtypeStruct((n, KV_WIDTH), F32)]
    return pl.pallas_call(
        functools.partial(_even_in_kernel, rope),
        grid=(n // tm,), in_specs=in_specs, out_specs=out_specs, out_shape=out_shape,
        compiler_params=_cparams(1), name="even_in_lat" if rope else "even_in_ctx",
    )(*args)


def _rope_tables(t_len):
    n = HEAD_DIM // 4
    inv = ROPE_BASE ** (-jnp.arange(n, dtype=F32) / n)
    t = jnp.arange(t_len)
    row = (t // GRID_W).astype(F32)
    col = (t % GRID_W).astype(F32)
    d = jnp.arange(LANES) % HEAD_DIM
    pos = jnp.where((d < HEAD_DIM // 2)[None, :], row[:, None], col[:, None])
    ang = pos * inv[d % n][None, :]
    sign = jnp.where((d % (2 * n)) < n, -1.0, 1.0).astype(F32)
    return jnp.cos(ang), jnp.sin(ang) * sign[None, :]


_NT = (((1,), (1,)), ((), ()))


def _attn_kernel(windowed, seq_len, *refs):
    if windowed:
        q_ref, kd_ref, vd_ref, ck_ref, cv_ref, sink_ref, o_ref = refs
    else:
        q_ref, ck_ref, cv_ref, sink_ref, o_ref = refs
    bq = q_ref.shape[0]
    group = N_HEADS // N_KV_HEADS
    lane_lo = lax.broadcasted_iota(jnp.int32, (bq, LANES), 1) < HEAD_DIM
    scale = HEAD_DIM ** -0.5
    if windowed:
        wk = bq + 2 * WINDOW
        qb = pl.program_id(1)
        ws = pl.multiple_of(jnp.clip(qb * bq - WINDOW, 0, seq_len - wk), LANES)
        qi = qb * bq + (lax.broadcasted_iota(jnp.int32, (group * bq, wk), 0) % bq)
        kj = ws + lax.broadcasted_iota(jnp.int32, (group * bq, wk), 1)
        in_band = jnp.abs(qi - kj) <= WINDOW
    for kh in range(N_KV_HEADS):
        base = kh * group * HEAD_DIM
        parts = []
        for j in range(group // 2):
            qs = q_ref[:, base + j * LANES:base + (j + 1) * LANES]
            zero = jnp.zeros_like(qs)
            parts += [jnp.where(lane_lo, qs, zero), jnp.where(lane_lo, zero, qs)]
        q4 = jnp.concatenate(parts, axis=0)
        sink = jnp.concatenate(
            [jnp.broadcast_to(sink_ref[kh * group + h:kh * group + h + 1, 0:1], (bq, 1)) for h in range(group)],
            axis=0)
        kc = ck_ref[:, kh * LANES:(kh + 1) * LANES]
        vc = cv_ref[:, kh * LANES:(kh + 1) * LANES]
        s_c = lax.dot_general(q4, kc, _NT, preferred_element_type=F32) * scale
        m = jnp.maximum(jnp.max(s_c, axis=-1, keepdims=True), sink)
        if windowed:
            kw = kd_ref[pl.ds(ws, wk), kh * LANES:(kh + 1) * LANES]
            vw = vd_ref[pl.ds(ws, wk), kh * LANES:(kh + 1) * LANES]
            s_w = lax.dot_general(q4, kw, _NT, preferred_element_type=F32) * scale
            s_w = jnp.where(in_band, s_w, NEG_INF)
            m = jnp.maximum(m, jnp.max(s_w, axis=-1, keepdims=True))
        p_c = jnp.exp(s_c - m)
        denom = jnp.sum(p_c, axis=-1, keepdims=True) + jnp.exp(sink - m)
        acc = jnp.dot(p_c.astype(BF16), vc, preferred_element_type=F32)
        if windowed:
            p_w = jnp.exp(s_w - m)
            denom = denom + jnp.sum(p_w, axis=-1, keepdims=True)
            acc = acc + jnp.dot(p_w.astype(BF16), vw, preferred_element_type=F32)
        o4 = acc * (1.0 / denom)
        for j in range(group // 2):
            slab = jnp.where(lane_lo, o4[2 * j * bq:(2 * j + 1) * bq], o4[(2 * j + 1) * bq:(2 * j + 2) * bq])
            o_ref[:, base + j * LANES:base + (j + 1) * LANES] = slab.astype(BF16)


def _attention_ctx(q, kd, vd, sink_tab, n_batch, seq_len):
    blk = lambda w: pl.BlockSpec((seq_len, w), lambda b: (b, 0))
    return pl.pallas_call(
        functools.partial(_attn_kernel, False, seq_len),
        grid=(n_batch,),
        in_specs=[blk(ATT_WIDTH), blk(2 * LANES), blk(2 * LANES), _full(sink_tab.shape)],
        out_specs=blk(ATT_WIDTH),
        out_shape=jax.ShapeDtypeStruct(q.shape, BF16),
        compiler_params=_cparams(1), name="attention_ctx",
    )(q, kd, vd, sink_tab)


def _attention_lat(q, kd, vd, ckd, cvd, sink_tab, n_batch, seq_len, ctx_len):
    nq = seq_len // ATT_BQ
    qspec = pl.BlockSpec((ATT_BQ, ATT_WIDTH), lambda b, i: (b * nq + i, 0))
    seq = pl.BlockSpec((seq_len, 2 * LANES), lambda b, i: (b, 0))
    ctx = pl.BlockSpec((ctx_len, 2 * LANES), lambda b, i: (b, 0))
    return pl.pallas_call(
        functools.partial(_attn_kernel, True, seq_len),
        grid=(n_batch, nq),
        in_specs=[qspec, seq, seq, ctx, ctx, _full(sink_tab.shape)],
        out_specs=qspec,
        out_shape=jax.ShapeDtypeStruct(q.shape, BF16),
        compiler_params=_cparams(2), name="attention_lat",
    )(q, kd, vd, ckd, cvd, sink_tab)


def _kv_dup_kernel(k_ref, v_ref, kd_ref, vd_ref):
    lane_lo = lax.broadcasted_iota(jnp.int32, k_ref.shape, 1) < HEAD_DIM
    k0, k1 = _dup_halves(k_ref[...], lane_lo)
    v0, v1 = _dup_halves(v_ref[...], lane_lo)
    kd_ref[:, 0:LANES] = k0.astype(BF16)
    kd_ref[:, LANES:2 * LANES] = k1.astype(BF16)
    vd_ref[:, 0:LANES] = v0.astype(BF16)
    vd_ref[:, LANES:2 * LANES] = v1.astype(BF16)


def _kv_dup(k, v):
    n = k.shape[0]
    out = jax.ShapeDtypeStruct((n, 2 * LANES), BF16)
    return pl.pallas_call(
        _kv_dup_kernel, grid=(1,),
        in_specs=[_full(k.shape), _full(v.shape)],
        out_specs=[_full((n, 2 * LANES)), _full((n, 2 * LANES))],
        out_shape=[out, out], compiler_params=_cparams(1), name="kv_dup",
    )(k, v)


def _conv_kernel(u_ref, w_ref, b_ref, g_ref, beta_ref, o_ref, xs_ref, z_ref):
    tc = o_ref.shape[0]
    c = pl.program_id(1)
    last = pl.num_programs(1) - 1
    r0 = pl.multiple_of(c * tc, tc)
    xs_ref[CONV_HALO:CONV_HALO + tc, :] = u_ref[pl.ds(r0, tc), :]
    lo = pl.multiple_of(jnp.maximum(r0 - CONV_HALO, 0), CONV_HALO)
    hi = pl.multiple_of(jnp.minimum(r0 + tc, u_ref.shape[0] - CONV_HALO), CONV_HALO)
    xs_ref[0:CONV_HALO, :] = jnp.where(c > 0, u_ref[pl.ds(lo, CONV_HALO), :], 0.0)
    xs_ref[CONV_HALO + tc:2 * CONV_HALO + tc, :] = jnp.where(c < last, u_ref[pl.ds(hi, CONV_HALO), :], 0.0)
    rb = 64
    off = CONV_HALO - CONV_K // 2
    for r in range(0, tc, rb):
        for s in range(0, CONV_CH, LANES):
            acc = jnp.zeros((rb, LANES), F32)
            for k in range(CONV_K):
                acc = acc + w_ref[k:k + 1, s:s + LANES] * xs_ref[r + off + k:r + off + k + rb, s:s + LANES]
            z_ref[r:r + rb, s:s + LANES] = acc + b_ref[:, s:s + LANES]
    z = z_ref[...]
    mu = jnp.mean(z, axis=-1, keepdims=True)
    d = z - mu
    var = jnp.mean(d * d, axis=-1, keepdims=True)
    y = d * lax.rsqrt(var + EPS) * g_ref[...] + beta_ref[...]
    o_ref[...] = (y * _sigmoid(y)).astype(BF16)


def _conformer_conv(u, n_batch, seq_len, w, b, g, beta):
    tc = min(CONV_TC, seq_len)
    nc = seq_len // tc
    return pl.pallas_call(
        _conv_kernel,
        grid=(n_batch, nc),
        in_specs=[pl.BlockSpec((seq_len, CONV_CH), lambda bb, c: (bb, 0)),
                  _full(w.shape), _full(b.shape), _full(g.shape), _full(beta.shape)],
        out_specs=pl.BlockSpec((tc, CONV_CH), lambda bb, c: (bb * nc + c, 0)),
        out_shape=jax.ShapeDtypeStruct(u.shape, BF16),
        scratch_shapes=[pltpu.VMEM((tc + 2 * CONV_HALO, CONV_CH), F32), pltpu.VMEM((tc, CONV_CH), F32)],
        compiler_params=_cparams(2), name="conformer_conv",
    )(u, w, b, g, beta)


def _out_mlp_kernel(n_mix, final, *refs):
    y_ref, mod_ref, g_ref = refs[0:3]
    mix_refs = refs[3:3 + n_mix]
    wo_refs = refs[3 + n_mix:3 + 2 * n_mix]
    rest = refs[3 + 2 * n_mix:]
    if final:
        w1_ref, w2_ref, gf_ref, o_ref = rest
    else:
        w1_ref, w2_ref, o_ref = rest
    proj = jnp.dot(mix_refs[0][...], wo_refs[0][...], preferred_element_type=F32)
    for m_ref, w_ref in zip(mix_refs[1:], wo_refs[1:]):
        proj = proj + jnp.dot(m_ref[...], w_ref[...], preferred_element_type=F32)
    y1 = y_ref[...] + mod_ref[2:3, :] * proj
    h = _norm_modulate(y1, g_ref[...], mod_ref[3:4, :], mod_ref[4:5, :]).astype(BF16)
    acc = jnp.zeros_like(y1)
    for f in range(0, D_FF, FF_CHUNK):
        t = jnp.maximum(jnp.dot(h, w1_ref[:, f:f + FF_CHUNK], preferred_element_type=F32), 0.0)
        acc = acc + jnp.dot((t * t).astype(BF16), w2_ref[f:f + FF_CHUNK, :], preferred_element_type=F32)
    y2 = y1 + mod_ref[5:6, :] * acc
    if final:
        ms = jnp.mean(y2 * y2, axis=-1, keepdims=True)
        y2 = y2 * lax.rsqrt(ms + EPS) * gf_ref[...]
    o_ref[...] = y2


def _out_mlp(y, mixes, mix_specs, w_outs, mod, layer, row_of_tile, g, w1, w2, final_g, tm):
    n = y.shape[0]
    row = lambda i: (i, 0)
    once = pl.Buffered(1)
    wspec = lambda a: pl.BlockSpec(a.shape, lambda i: (0, 0), pipeline_mode=once)
    in_specs = ([pl.BlockSpec((tm, D_MODEL), row), _mod_spec(layer, row_of_tile), _full((1, D_MODEL))]
                + list(mix_specs) + [wspec(w) for w in w_outs] + [wspec(w1), wspec(w2)])
    args = [y, mod, g] + list(mixes) + list(w_outs) + [w1, w2]
    if final_g is not None:
        in_specs.append(_full((1, D_MODEL)))
        args.append(final_g)
    return pl.pallas_call(
        functools.partial(_out_mlp_kernel, len(mixes), final_g is not None),
        grid=(n // tm,), in_specs=in_specs,
        out_specs=pl.BlockSpec((tm, D_MODEL), row),
        out_shape=jax.ShapeDtypeStruct((n, D_MODEL), F32),
        compiler_params=_cparams(1), name="out_mlp",
    )(*args)


def _odd_in_kernel(x_ref, mod_ref, g_ref, w_ref, gate_ref, rec_ref):
    h = _norm_modulate(x_ref[...], g_ref[...], mod_ref[0:1, :], mod_ref[1:2, :]).astype(BF16)
    gate_ref[...] = jnp.dot(h, w_ref[:, 0:LRU_WIDTH], preferred_element_type=F32).astype(BF16)
    rec_ref[...] = jnp.dot(h, w_ref[:, LRU_WIDTH:2 * LRU_WIDTH], preferred_element_type=F32).astype(BF16)


def _odd_in(x, mod, layer, row_of_tile, g, w_in, n_group, seq_len):
    tm = ROW_TILE
    tiles_per_seq = seq_len // tm

    def out_map(i):
        s = i // tiles_per_seq
        return (s // LRU_SEQS, i % tiles_per_seq, s % LRU_SEQS)

    ospec = pl.BlockSpec((None, tm, LRU_WIDTH), out_map)
    oshape = jax.ShapeDtypeStruct((n_group, seq_len, LRU_SEQS * LRU_WIDTH), BF16)
    gate, rec = pl.pallas_call(
        _odd_in_kernel,
        grid=(x.shape[0] // tm,),
        in_specs=[pl.BlockSpec((tm, D_MODEL), lambda i: (i, 0)), _mod_spec(layer, row_of_tile),
                  _full((1, D_MODEL)), _full(w_in.shape)],
        out_specs=[ospec, ospec], out_shape=[oshape, oshape],
        compiler_params=_cparams(1), name="odd_in",
    )(x, mod, g, w_in)
    shape = (n_group, seq_len * LRU_SEQS, LRU_WIDTH)
    return gate.reshape(shape), rec.reshape(shape)


def _gelu_tanh(x):
    return 0.5 * x * (1.0 + jnp.tanh(math.sqrt(2.0 / math.pi) * (x + 0.044715 * (x * x * x))))


def _lru_kernel(reverse, *refs):
    if reverse:
        (rec_ref, prev_ref, next_ref, gate_ref, hf_ref, cw_ref, cb_ref, wbd_ref, ba_ref, bx_ref, lam_ref, h0_ref,
         out_ref, state_ref, xs_ref, xc_ref, a_ref, b_ref, hout_ref, h_ref) = refs
    else:
        (rec_ref, prev_ref, next_ref, cw_ref, cb_ref, wbd_ref, ba_ref, bx_ref, lam_ref, h0_ref,
         out_ref, state_ref, xs_ref, xc_ref, a_ref, b_ref, h_ref) = refs
        hout_ref = out_ref
    rows = rec_ref.shape[0]
    c = pl.program_id(1)
    n_chunks = pl.num_programs(1)
    tchunk = (n_chunks - 1 - c) if reverse else c

    @pl.when(c == 0)
    def _():
        h_ref[...] = h0_ref[...]

    xs_ref[SUBLANES:SUBLANES + rows, :] = rec_ref[...].astype(F32)
    xs_ref[0:SUBLANES, :] = jnp.where(tchunk > 0, prev_ref[LRU_HALO - SUBLANES:LRU_HALO, :].astype(F32), 0.0)
    xs_ref[SUBLANES + rows:SUBLANES + rows + LRU_HALO, :] = jnp.where(
        tchunk < n_chunks - 1, next_ref[...].astype(F32), 0.0)
    xc = cb_ref[...] + cw_ref[0:1, :] * xs_ref[4:4 + rows, :]
    for k in range(1, LRU_CONV_K):
        xc = xc + cw_ref[k:k + 1, :] * xs_ref[4 + LRU_SEQS * k:4 + LRU_SEQS * k + rows, :]
    xc_ref[...] = xc

    pair = 2 * LRU_BW
    sp = jnp.log(1.0 + jnp.exp(-lam_ref[...]))
    for j in range(LRU_BLOCKS // 2):
        sl = slice(j * pair, (j + 1) * pair)
        x = xc_ref[:, sl]
        pre = jnp.dot(x.astype(BF16), wbd_ref[j], preferred_element_type=F32)
        r = _sigmoid(pre[:, 0:pair] + ba_ref[:, sl])
        i = _sigmoid(pre[:, pair:2 * pair] + bx_ref[:, sl])
        log_a = (-LRU_C) * r * sp[:, sl]
        u = jnp.tanh(0.5 * log_a)
        inv = 1.0 / (1.0 - u)
        a = (1.0 + u) * inv
        a_ref[:, sl] = a
        b_ref[:, sl] = jnp.sqrt((-2.0 * u * inv) * (a + 1.0)) * (i * x)

    n_tiles = rows // SUBLANES
    low = lax.broadcasted_iota(jnp.int32, (SUBLANES, LRU_WIDTH), 0) < LRU_SEQS

    def step(t, h):
        tile = (n_tiles - 1 - t) if reverse else t
        r0 = pl.multiple_of(tile * SUBLANES, SUBLANES)
        a = a_ref[pl.ds(r0, SUBLANES), :]
        b = b_ref[pl.ds(r0, SUBLANES), :]
        h1 = a * h + b
        h2 = a * pltpu.roll(h1, LRU_SEQS, axis=0) + b
        first, second = (h2, h1) if reverse else (h1, h2)
        hout_ref[pl.ds(r0, SUBLANES), :] = jnp.where(low, first, second)
        return pltpu.roll(h2, LRU_SEQS, axis=0)

    h_fin = lax.fori_loop(0, n_tiles, step, h_ref[...], unroll=2)
    h_ref[...] = h_fin
    state_ref[...] = h_fin
    if reverse:
        y = (hf_ref[...] + hout_ref[...]) * _gelu_tanh(gate_ref[...].astype(F32))
        out_ref[...] = y.astype(BF16)


def _lru_direction(reverse, rec, gate, hf, h0, cw, cb, wbd, ba, bx, lam):
    n_group, n_rows, _ = rec.shape
    rows = LRU_ROWS
    nc = n_rows // rows
    halo_blocks = n_rows // LRU_HALO
    per = rows // LRU_HALO
    pos = (lambda c: nc - 1 - c) if reverse else (lambda c: c)
    main = pl.BlockSpec((None, rows, LRU_WIDTH), lambda g, c: (g, pos(c), 0))
    prev = pl.BlockSpec((None, LRU_HALO, LRU_WIDTH), lambda g, c: (g, jnp.maximum(pos(c) * per - 1, 0), 0))
    nxt = pl.BlockSpec((None, LRU_HALO, LRU_WIDTH),
                       lambda g, c: (g, jnp.minimum((pos(c) + 1) * per, halo_blocks - 1), 0))
    state_spec = pl.BlockSpec((None, SUBLANES, LRU_WIDTH), lambda g, c: (g, 0, 0))
    in_specs = [main, prev, nxt]
    args = [rec, rec, rec]
    if reverse:
        in_specs += [main, main]
        args += [gate, hf]
    in_specs += [_full(cw.shape), _full(cb.shape), _full(wbd.shape), _full(ba.shape), _full(bx.shape),
                 _full(lam.shape), state_spec]
    args += [cw, cb, wbd, ba, bx, lam, h0]
    scratch = [pltpu.VMEM((rows + SUBLANES + LRU_HALO, LRU_WIDTH), F32),
               pltpu.VMEM((rows, LRU_WIDTH), F32),
               pltpu.VMEM((rows, LRU_WIDTH), F32),
               pltpu.VMEM((rows, LRU_WIDTH), F32)]
    if reverse:
        scratch.append(pltpu.VMEM((rows, LRU_WIDTH), F32))
    scratch.append(pltpu.VMEM((SUBLANES, LRU_WIDTH), F32))
    return pl.pallas_call(
        functools.partial(_lru_kernel, reverse),
        grid=(n_group, nc), in_specs=in_specs,
        out_specs=[main, state_spec],
        out_shape=[jax.ShapeDtypeStruct(rec.shape, BF16 if reverse else F32),
                   jax.ShapeDtypeStruct((n_group, SUBLANES, LRU_WIDTH), F32)],
        scratch_shapes=scratch,
        compiler_params=_cparams(2), name="lru_bwd" if reverse else "lru_fwd",
    )(*args)


def _gate_weights(wa, wx):
    z = jnp.zeros((LRU_BW, LRU_BW), wa.dtype)
    mats = []
    for j in range(LRU_BLOCKS // 2):
        top = jnp.concatenate([wa[2 * j], z, wx[2 * j], z], axis=1)
        bot = jnp.concatenate([z, wa[2 * j + 1], z, wx[2 * j + 1]], axis=1)
        mats.append(jnp.concatenate([top, bot], axis=0))
    return jnp.stack(mats).astype(BF16)


def _lru_core(rec, gate, h0_f, h0_b, cw, cb, wa, ba, wx, bx, lam):
    zeros = jnp.zeros_like(h0_f)
    row = lambda v: v.reshape(1, LRU_WIDTH)
    hf, st_f = _lru_direction(False, rec, None, None, jnp.concatenate([h0_f, zeros], axis=1),
                              cw, row(cb), _gate_weights(wa[0], wx[0]), row(ba[0]), row(bx[0]), row(lam[0]))
    y, st_b = _lru_direction(True, rec, gate, hf, jnp.concatenate([zeros, h0_b], axis=1),
                             cw, row(cb), _gate_weights(wa[1], wx[1]), row(ba[1]), row(bx[1]), row(lam[1]))
    return y, st_f[:, :LRU_SEQS], st_b[:, LRU_SEQS:]


def kernel(x_prompt, x_sample, c, cache_k, cache_v, state_lru, c_ctx, w_mod, b_mod, norm_mix, norm_ffn, w_ff1, w_ff2, att_in, att_out, att_sink, conv_w, conv_b, conv_norm_g, conv_norm_b, lru_in, lru_out, lru_conv_w, lru_conv_b, lru_wa, lru_ba, lru_wx, lru_bx, lru_lam, final_norm):
    nb_p, t_p, _ = x_prompt.shape
    nb_s, t_s, _ = x_sample.shape
    depth = w_mod.shape[0]
    assert nb_s == LRU_SEQS and nb_p % LRU_SEQS == 0 and nb_s + 1 <= MOD_ROWS
    assert t_p % ROW_TILE == 0 and t_s % ROW_TILE == 0

    cond = jnp.concatenate([c_ctx[None, :], c, jnp.zeros((MOD_ROWS - 1 - nb_s, D_MODEL), F32)], axis=0)
    mod = _modulation(cond, w_mod, b_mod)

    yp = x_prompt.reshape(nb_p * t_p, D_MODEL)
    ys = x_sample.reshape(nb_s * t_s, D_MODEL)
    bf = lambda w: w.astype(BF16)
    vec = lambda v: v.reshape(1, -1)

    def mod_row_p(tm):
        return lambda i: 0

    def mod_row_s(tm):
        return lambda i: 1 + (i * tm) // t_s

    new_k, new_v, new_h = [], [], []
    for li in range(depth):
        last = li == depth - 1
        final_g = vec(final_norm) if last else None
        w1, w2 = bf(w_ff1[li]), bf(w_ff2[li])
        if li % 2 == 0:
            e = li // 2
            w_in, w_out = bf(att_in[e]), bf(att_out[e])
            sink_tab = jnp.broadcast_to(att_sink[e][:, None], (N_HEADS, LANES))
            cpar = (conv_w[e], vec(conv_b[e]), vec(conv_norm_g[e]), vec(conv_norm_b[e]))
            q, kd, vd, u, k, v = _even_in(yp, mod, li, mod_row_p(ROW_TILE), vec(norm_mix[li]), w_in, None)
            new_k.append(k.reshape(nb_p, t_p, N_KV_HEADS, HEAD_DIM))
            new_v.append(v.reshape(nb_p, t_p, N_KV_HEADS, HEAD_DIM))
            o_att = _attention_ctx(q, kd, vd, sink_tab, nb_p, t_p)
            o_conv = _conformer_conv(u, nb_p, t_p, *cpar)
            mix_specs = [pl.BlockSpec((MLP_ROW_TILE, ATT_WIDTH), lambda i: (i, 0)),
                         pl.BlockSpec((MLP_ROW_TILE, CONV_CH), lambda i: (i, 0))]
            w_outs = [w_out[:ATT_WIDTH], w_out[ATT_WIDTH:]]
            yp = _out_mlp(yp, [o_att, o_conv], mix_specs, w_outs, mod, li, mod_row_p(MLP_ROW_TILE),
                          vec(norm_ffn[li]), w1, w2, final_g, MLP_ROW_TILE)
            past = cache_k.shape[2]
            ckd, cvd = _kv_dup(cache_k[:, e].reshape(nb_s * past, KV_WIDTH),
                               cache_v[:, e].reshape(nb_s * past, KV_WIDTH))
            q, kd, vd, u = _even_in(ys, mod, li, mod_row_s(ROW_TILE), vec(norm_mix[li]), w_in, _rope_tables(t_s))
            o_att = _attention_lat(q, kd, vd, ckd, cvd, sink_tab, nb_s, t_s, past)
            o_conv = _conformer_conv(u, nb_s, t_s, *cpar)
            ys = _out_mlp(ys, [o_att, o_conv], mix_specs, w_outs, mod, li, mod_row_s(MLP_ROW_TILE),
                          vec(norm_ffn[li]), w1, w2, final_g, MLP_ROW_TILE)
        else:
            o = li // 2
            w_in, w_out = bf(lru_in[o]), bf(lru_out[o])
            lpar = (lru_conv_w[o], lru_conv_b[o], lru_wa[o], lru_ba[o], lru_wx[o], lru_bx[o], lru_lam[o])
            tm = ROW_TILE

            def lru_stream(y, mod_row, n_batch, seq_len, h0_f, h0_b):
                n_group = n_batch // LRU_SEQS
                gate, rec = _odd_in(y, mod, li, mod_row(ROW_TILE), vec(norm_mix[li]), w_in, n_group, seq_len)
                y_lru, st_f, st_b = _lru_core(rec, gate, h0_f, h0_b, *lpar)
                tiles_per_seq = seq_len // tm

                def mix_map(i):
                    s = i // tiles_per_seq
                    return (s // LRU_SEQS, i % tiles_per_seq, s % LRU_SEQS)

                mix = y_lru.reshape(n_group, seq_len, LRU_SEQS * LRU_WIDTH)
                mix_spec = pl.BlockSpec((None, tm, LRU_WIDTH), mix_map)
                y_new = _out_mlp(y, [mix], [mix_spec], [w_out], mod, li, mod_row(tm),
                                 vec(norm_ffn[li]), w1, w2, final_g, tm)
                return y_new, st_f, st_b

            zeros_p = jnp.zeros((nb_p // LRU_SEQS, LRU_SEQS, LRU_WIDTH), F32)
            yp, st_f, st_b = lru_stream(yp, mod_row_p, nb_p, t_p, zeros_p, zeros_p)
            new_h.append(jnp.stack([st_f.reshape(nb_p, LRU_WIDTH), st_b.reshape(nb_p, LRU_WIDTH)], axis=1))
            ys, _, _ = lru_stream(ys, mod_row_s, nb_s, t_s,
                                  state_lru[:, o, 0][None].astype(F32), state_lru[:, o, 1][None].astype(F32))

    y_prompt = yp.reshape(nb_p, t_p, D_MODEL)
    y_sample = ys.reshape(nb_s, t_s, D_MODEL)
    return (y_prompt, y_sample, jnp.stack(new_k, axis=1), jnp.stack(new_v, axis=1), jnp.stack(new_h, axis=1))
```

```python
import functools
import math

import jax
import jax.numpy as jnp
import numpy as np
from jax import lax
from jax.experimental import pallas as pl
from jax.experimental.pallas import tpu as pltpu

F32 = jnp.float32
BF16 = jnp.bfloat16

D_MODEL = 1024
N_HEADS = 8
N_KV_HEADS = 2
HEAD_DIM = 64
ATT_WIDTH = N_HEADS * HEAD_DIM
KV_WIDTH = N_KV_HEADS * HEAD_DIM
WINDOW = 128
GRID_W = 64
ROPE_BASE = 10000.0
CONV_CH = 512
CONV_K = 31
LRU_WIDTH = 1024
LRU_BLOCKS = 8
LRU_BW = LRU_WIDTH // LRU_BLOCKS
LRU_CONV_K = 4
LRU_C = 8.0
D_FF = 4 * D_MODEL
EPS = 1e-6
NEG_INF = -1e30

LANES = 128
SUBLANES = 8
VMEM_LIMIT_BYTES = 56 * 1024 * 1024

MOD_ROWS = 8
LRU_SEQS = 4
LRU_TSTEPS = 64
ROW_TILE = 256
MLP_ROW_TILE = 512
FF_CHUNK = 1024
ATT_BQ = 128
CONV_TC = 128
CONV_RB = 64
CONV_HALO = 16
LRU_ROWS = 512
LRU_HALO = 16


def _cparams(n_axes):
    return pltpu.CompilerParams(dimension_semantics=("arbitrary",) * n_axes,
                                vmem_limit_bytes=VMEM_LIMIT_BYTES)


def _full(shape):
    n = len(shape)
    return pl.BlockSpec(shape, lambda *_: (0,) * n)


def _norm_modulate(x, g, shift, scale):
    ms = jnp.mean(x * x, axis=-1, keepdims=True)
    return (x * lax.rsqrt(ms + EPS) * g) * (1.0 + scale) + shift


def _sigmoid(x):
    return 1.0 / (1.0 + jnp.exp(-x))


def _gelu_tanh(x):
    return 0.5 * x * (1.0 + jnp.tanh(math.sqrt(2.0 / math.pi) * (x + 0.044715 * (x * x * x))))


def _mod_kernel(cond_ref, w_ref, b_ref, o_ref):
    c = cond_ref[...]
    s = (c * _sigmoid(c)).astype(BF16)
    o_ref[...] = jnp.dot(s, w_ref[...].astype(BF16), preferred_element_type=F32) + b_ref[...]


def _modulation(cond, w_mod, b_mod):
    depth, _, n6 = w_mod.shape
    tn = n6 // 4
    out = pl.pallas_call(
        _mod_kernel,
        grid=(depth, n6 // tn),
        in_specs=[_full((MOD_ROWS, D_MODEL)),
                  pl.BlockSpec((None, D_MODEL, tn), lambda l, j: (l, 0, j)),
                  pl.BlockSpec((None, 1, tn), lambda l, j: (l, 0, j))],
        out_specs=pl.BlockSpec((None, MOD_ROWS, tn), lambda l, j: (l, 0, j)),
        out_shape=jax.ShapeDtypeStruct((depth, MOD_ROWS, n6), F32),
        compiler_params=_cparams(2),
        name="modulation",
    )(cond, w_mod, b_mod.reshape(depth, 1, n6))
    return out.reshape(depth, MOD_ROWS, 6, D_MODEL)


def _mod_spec(layer, row_of_tile):
    return pl.BlockSpec((None, None, 6, D_MODEL), lambda i: (layer, row_of_tile(i), 0, 0))


def _mod_group_spec(layer, n_rows, first_row):
    return pl.BlockSpec((None, n_rows, 6, D_MODEL), lambda i: (layer, first_row // n_rows, 0, 0))


def _rope_slab(x, cos, sin_signed, first16):
    partner = jnp.where(first16, pltpu.roll(x, LANES - 16, axis=1), pltpu.roll(x, 16, axis=1))
    return x * cos + partner * sin_signed


def _even_in_kernel(rope, *refs):
    if rope:
        x_ref, mod_ref, g_ref, w_ref, cos_ref, sin_ref, q_ref, k_ref, vt_ref, u_ref = refs
    else:
        x_ref, mod_ref, g_ref, w_ref, q_ref, k_ref, vt_ref, u_ref, k32_ref, v32_ref = refs
    h = _norm_modulate(x_ref[...], g_ref[...], mod_ref[0:1, :], mod_ref[1:2, :]).astype(BF16)
    tm = h.shape[0]
    lane = lax.broadcasted_iota(jnp.int32, (tm, LANES), 1)
    lane_lo = lane < HEAD_DIM
    q = jnp.dot(h, w_ref[:, 0:ATT_WIDTH], preferred_element_type=F32) * (HEAD_DIM ** -0.5)
    kv = jnp.dot(h, w_ref[:, ATT_WIDTH:ATT_WIDTH + 2 * KV_WIDTH], preferred_element_type=F32)
    k = kv[:, 0:KV_WIDTH]
    v = kv[:, KV_WIDTH:2 * KV_WIDTH]
    if rope:
        cos = cos_ref[...]
        sin = sin_ref[...]
        first16 = (lane % 32) < 16
        k = _rope_slab(k, cos, sin, first16)
    else:
        k32_ref[...] = k
        v32_ref[...] = v
    k_ref[...] = k.astype(BF16)
    vt_ref[...] = v.T.astype(BF16)
    heads_per_kv = N_HEADS // N_KV_HEADS
    for j in range(ATT_WIDTH // LANES):
        qs = q[:, j * LANES:(j + 1) * LANES]
        if rope:
            qs = _rope_slab(qs, cos, sin, first16)
        sw = pltpu.roll(qs, HEAD_DIM, axis=1)
        if (2 * j) // heads_per_kv == 0:
            even, odd = jnp.where(lane_lo, qs, 0.0), jnp.where(lane_lo, sw, 0.0)
        else:
            even, odd = jnp.where(lane_lo, 0.0, sw), jnp.where(lane_lo, 0.0, qs)
        q_ref[:, 2 * j * LANES:(2 * j + 1) * LANES] = even.astype(BF16)
        q_ref[:, (2 * j + 1) * LANES:(2 * j + 2) * LANES] = odd.astype(BF16)
    u0 = ATT_WIDTH + 2 * KV_WIDTH
    ua = jnp.dot(h, w_ref[:, u0:u0 + CONV_CH], preferred_element_type=F32)
    ug = jnp.dot(h, w_ref[:, u0 + CONV_CH:u0 + 2 * CONV_CH], preferred_element_type=F32)
    u_ref[...] = ua * _sigmoid(ug)


def _even_in(x, mod, layer, row_of_tile, g, w_in, n_batch, seq_len, rope_tabs):
    n = x.shape[0]
    tm = ROW_TILE
    rope = rope_tabs is not None
    tiles_per_seq = seq_len // tm
    row = lambda i: (i, 0)
    vt_spec = pl.BlockSpec((None, KV_WIDTH, tm), lambda i: (i // tiles_per_seq, 0, i % tiles_per_seq))
    in_specs = [pl.BlockSpec((tm, D_MODEL), row), _mod_spec(layer, row_of_tile),
                _full((1, D_MODEL)), _full(w_in.shape)]
    args = [x, mod, g, w_in]
    out_specs = [pl.BlockSpec((tm, N_HEADS * LANES), row), pl.BlockSpec((tm, KV_WIDTH), row),
                 vt_spec, pl.BlockSpec((tm, CONV_CH), row)]
    out_shape = [jax.ShapeDtypeStruct((n, N_HEADS * LANES), BF16), jax.ShapeDtypeStruct((n, KV_WIDTH), BF16),
                 jax.ShapeDtypeStruct((n_batch, KV_WIDTH, seq_len), BF16),
                 jax.ShapeDtypeStruct((n, CONV_CH), F32)]
    if rope:
        cos, sin = rope_tabs
        tab = lambda i: (i % tiles_per_seq, 0)
        in_specs += [pl.BlockSpec((tm, LANES), tab), pl.BlockSpec((tm, LANES), tab)]
        args += [cos, sin]
    else:
        out_specs += [pl.BlockSpec((tm, KV_WIDTH), row), pl.BlockSpec((tm, KV_WIDTH), row)]
        out_shape += [jax.ShapeDtypeStruct((n, KV_WIDTH), F32), jax.ShapeDtypeStruct((n, KV_WIDTH), F32)]
    return pl.pallas_call(
        functools.partial(_even_in_kernel, rope),
        grid=(n // tm,), in_specs=in_specs, out_specs=out_specs, out_shape=out_shape,
        compiler_params=_cparams(1), name="even_in_lat" if rope else "even_in_ctx",
    )(*args)


def _rope_tables(t_len):
    n = HEAD_DIM // 4
    inv = ROPE_BASE ** (-jnp.arange(n, dtype=F32) / n)
    t = jnp.arange(t_len)
    row = (t // GRID_W).astype(F32)
    col = (t % GRID_W).astype(F32)
    d = jnp.arange(LANES) % HEAD_DIM
    pos = jnp.where((d < HEAD_DIM // 2)[None, :], row[:, None], col[:, None])
    ang = pos * inv[d % n][None, :]
    sign = jnp.where((d % (2 * n)) < n, -1.0, 1.0).astype(F32)
    return jnp.cos(ang), jnp.sin(ang) * sign[None, :]


_NT = (((1,), (1,)), ((), ()))


def _attn_kernel(windowed, seq_len, *refs):
    if windowed:
        q_ref, k_ref, vt_ref, ck_ref, cvt_ref, sink_ref, o_ref = refs
    else:
        q_ref, ck_ref, cvt_ref, sink_ref, o_ref = refs
    bq = q_ref.shape[0]
    group = N_HEADS // N_KV_HEADS
    if windowed:
        wk = bq + 2 * WINDOW
        qb = pl.program_id(1)
        ws = pl.multiple_of(jnp.clip(qb * bq - WINDOW, 0, seq_len - wk), LANES)
        kj = ws + lax.broadcasted_iota(jnp.int32, (wk, bq), 0)
        qi = qb * bq + lax.broadcasted_iota(jnp.int32, (wk, bq), 1)
        bias1 = jnp.where(jnp.abs(qi - kj) <= WINDOW, 0.0, NEG_INF)
        bias = jnp.concatenate([bias1] * group, axis=1)
        kw = k_ref[pl.ds(ws, wk), :]
        vtw = vt_ref[:, pl.ds(ws, wk)]
    kc = ck_ref[...]
    vtc = cvt_ref[...]
    for kh in range(N_KV_HEADS):
        q4 = jnp.concatenate([q_ref[:, (kh * group + h) * LANES:(kh * group + h + 1) * LANES]
                              for h in range(group)], axis=0)
        sink = jnp.concatenate([sink_ref[kh * group + h:kh * group + h + 1, :] for h in range(group)
                                for _ in range(bq // LANES)], axis=1)
        s_c = lax.dot_general(kc, q4, _NT, preferred_element_type=F32)
        m = jnp.maximum(jnp.max(s_c, axis=0, keepdims=True), sink)
        if windowed:
            s_w = lax.dot_general(kw, q4, _NT, preferred_element_type=F32) + bias
            m = jnp.maximum(m, jnp.max(s_w, axis=0, keepdims=True))
        p_c = jnp.exp(s_c - m)
        denom = jnp.sum(p_c, axis=0, keepdims=True) + jnp.exp(sink - m)
        acc = jnp.dot(vtc, p_c.astype(BF16), preferred_element_type=F32)
        if windowed:
            p_w = jnp.exp(s_w - m)
            denom = denom + jnp.sum(p_w, axis=0, keepdims=True)
            acc = acc + jnp.dot(vtw, p_w.astype(BF16), preferred_element_type=F32)
        ot = acc[kh * HEAD_DIM:(kh + 1) * HEAD_DIM, :] * (1.0 / denom)
        for j in range(group // 2):
            z = jnp.concatenate([ot[:, 2 * j * bq:(2 * j + 1) * bq], ot[:, (2 * j + 1) * bq:(2 * j + 2) * bq]],
                                axis=0)
            base = kh * group * HEAD_DIM + j * LANES
            o_ref[:, base:base + LANES] = z.T.astype(BF16)


def _attention_ctx(q, k, vt, sink_tab, n_batch, seq_len):
    return pl.pallas_call(
        functools.partial(_attn_kernel, False, seq_len),
        grid=(n_batch,),
        in_specs=[pl.BlockSpec((seq_len, N_HEADS * LANES), lambda b: (b, 0)),
                  pl.BlockSpec((seq_len, KV_WIDTH), lambda b: (b, 0)),
                  pl.BlockSpec((None, KV_WIDTH, seq_len), lambda b: (b, 0, 0)),
                  _full(sink_tab.shape)],
        out_specs=pl.BlockSpec((seq_len, ATT_WIDTH), lambda b: (b, 0)),
        out_shape=jax.ShapeDtypeStruct((n_batch * seq_len, ATT_WIDTH), BF16),
        compiler_params=_cparams(1), name="attention_ctx",
    )(q, k, vt, sink_tab)


def _attention_lat(q, k, vt, ck, cvt, sink_tab, n_batch, seq_len, ctx_len):
    nq = seq_len // ATT_BQ
    return pl.pallas_call(
        functools.partial(_attn_kernel, True, seq_len),
        grid=(n_batch, nq),
        in_specs=[pl.BlockSpec((ATT_BQ, N_HEADS * LANES), lambda b, i: (b * nq + i, 0)),
                  pl.BlockSpec((seq_len, KV_WIDTH), lambda b, i: (b, 0)),
                  pl.BlockSpec((None, KV_WIDTH, seq_len), lambda b, i: (b, 0, 0)),
                  pl.BlockSpec((ctx_len, KV_WIDTH), lambda b, i: (b, 0)),
                  pl.BlockSpec((None, KV_WIDTH, ctx_len), lambda b, i: (b, 0, 0)),
                  _full(sink_tab.shape)],
        out_specs=pl.BlockSpec((ATT_BQ, ATT_WIDTH), lambda b, i: (b * nq + i, 0)),
        out_shape=jax.ShapeDtypeStruct((n_batch * seq_len, ATT_WIDTH), BF16),
        compiler_params=_cparams(2), name="attention_lat",
    )(q, k, vt, ck, cvt, sink_tab)


def _kv_prep_kernel(k_ref, v_ref, kb_ref, vt_ref):
    kb_ref[...] = k_ref[...].astype(BF16)
    vt_ref[...] = v_ref[...].T.astype(BF16)


def _kv_prep(k, v, n_batch, ctx_len):
    row = pl.BlockSpec((ctx_len, KV_WIDTH), lambda b: (b, 0))
    return pl.pallas_call(
        _kv_prep_kernel, grid=(n_batch,),
        in_specs=[row, row],
        out_specs=[row, pl.BlockSpec((None, KV_WIDTH, ctx_len), lambda b: (b, 0, 0))],
        out_shape=[jax.ShapeDtypeStruct(k.shape, BF16), jax.ShapeDtypeStruct((n_batch, KV_WIDTH, ctx_len), BF16)],
        compiler_params=_cparams(1), name="kv_prep",
    )(k, v)


def _conv_kernel(u_ref, w_ref, b_ref, g_ref, beta_ref, o_ref, xs_ref, z_ref):
    tc = o_ref.shape[0]
    c = pl.program_id(1)
    last = pl.num_programs(1) - 1
    r0 = pl.multiple_of(c * tc, tc)
    xs_ref[CONV_HALO:CONV_HALO + tc, :] = u_ref[pl.ds(r0, tc), :]
    lo = pl.multiple_of(jnp.maximum(r0 - CONV_HALO, 0), CONV_HALO)
    hi = pl.multiple_of(jnp.minimum(r0 + tc, u_ref.shape[0] - CONV_HALO), CONV_HALO)
    xs_ref[0:CONV_HALO, :] = jnp.where(c > 0, u_ref[pl.ds(lo, CONV_HALO), :], 0.0)
    xs_ref[CONV_HALO + tc:2 * CONV_HALO + tc, :] = jnp.where(c < last, u_ref[pl.ds(hi, CONV_HALO), :], 0.0)
    off = CONV_HALO - CONV_K // 2
    rb = CONV_RB
    for r in range(0, tc, rb):
        for s in range(0, CONV_CH, LANES):
            acc = None
            for rho in range(SUBLANES):
                part = None
                for k in range(CONV_K):
                    if (off + k) % SUBLANES != rho:
                        continue
                    a0 = r + (off + k) - rho
                    term = w_ref[k:k + 1, s:s + LANES] * xs_ref[a0:a0 + rb + SUBLANES, s:s + LANES]
                    part = term if part is None else part + term
                part = part[rho:rho + rb]
                acc = part if acc is None else acc + part
            z_ref[r:r + rb, s:s + LANES] = acc + b_ref[:, s:s + LANES]
    z = z_ref[...]
    mu = jnp.mean(z, axis=-1, keepdims=True)
    d = z - mu
    var = jnp.mean(d * d, axis=-1, keepdims=True)
    y = d * lax.rsqrt(var + EPS) * g_ref[...] + beta_ref[...]
    o_ref[...] = (y * _sigmoid(y)).astype(BF16)


def _conformer_conv(u, n_batch, seq_len, w, b, g, beta):
    tc = min(CONV_TC, seq_len)
    nc = seq_len // tc
    return pl.pallas_call(
        _conv_kernel,
        grid=(n_batch, nc),
        in_specs=[pl.BlockSpec((seq_len, CONV_CH), lambda bb, c: (bb, 0)),
                  _full(w.shape), _full(b.shape), _full(g.shape), _full(beta.shape)],
        out_specs=pl.BlockSpec((tc, CONV_CH), lambda bb, c: (bb * nc + c, 0)),
        out_shape=jax.ShapeDtypeStruct(u.shape, BF16),
        scratch_shapes=[pltpu.VMEM((tc + 2 * CONV_HALO + SUBLANES, CONV_CH), F32), pltpu.VMEM((tc, CONV_CH), F32)],
        compiler_params=_cparams(2), name="conformer_conv",
    )(u, w, b, g, beta)


def _out_mlp_kernel(n_mix, grouped, final, *refs):
    y_ref, mod_ref, g_ref = refs[0:3]
    mix_refs = refs[3:3 + n_mix]
    wo_refs = refs[3 + n_mix:3 + 2 * n_mix]
    rest = list(refs[3 + 2 * n_mix:])
    o_ref = rest.pop()
    gf_ref = rest.pop() if final else None
    w1_ref, w2_ref = rest[0:2]
    y = y_ref[...]
    if grouped:
        perm_ref = rest[2]
        mix = jnp.dot(perm_ref[...], mix_refs[0][...], preferred_element_type=F32).astype(BF16)
        proj = jnp.dot(mix, wo_refs[0][...], preferred_element_type=F32).reshape(y.shape)
        mrow = lambda k: mod_ref[:, k:k + 1, :]
    else:
        proj = jnp.dot(mix_refs[0][...], wo_refs[0][...], preferred_element_type=F32)
        for m_ref, w_ref in zip(mix_refs[1:], wo_refs[1:]):
            proj = proj + jnp.dot(m_ref[...], w_ref[...], preferred_element_type=F32)
        mrow = lambda k: mod_ref[k:k + 1, :]
    y1 = y + mrow(2) * proj
    h = _norm_modulate(y1, g_ref[...], mrow(3), mrow(4)).astype(BF16).reshape(-1, D_MODEL)
    acc = jnp.zeros(h.shape, F32)
    for f in range(0, D_FF, FF_CHUNK):
        t = jnp.maximum(jnp.dot(h, w1_ref[:, f:f + FF_CHUNK], preferred_element_type=F32), 0.0)
        acc = acc + jnp.dot((t * t).astype(BF16), w2_ref[f:f + FF_CHUNK, :], preferred_element_type=F32)
    y2 = y1 + mrow(5) * acc.reshape(y.shape)
    if final:
        ms = jnp.mean(y2 * y2, axis=-1, keepdims=True)
        y2 = y2 * lax.rsqrt(ms + EPS) * gf_ref[...]
    o_ref[...] = y2


def _out_mlp(y, y_spec, grid, mixes, mix_specs, w_outs, mod, mod_spec, g, w1, w2, perm, final_g):
    once = pl.Buffered(1)
    wspec = lambda a: pl.BlockSpec(a.shape, lambda i: (0, 0), pipeline_mode=once)
    in_specs = ([y_spec, mod_spec, _full((1, D_MODEL))]
                + list(mix_specs) + [wspec(w) for w in w_outs] + [wspec(w1), wspec(w2)])
    args = [y, mod, g] + list(mixes) + list(w_outs) + [w1, w2]
    if perm is not None:
        in_specs.append(_full(perm.shape))
        args.append(perm)
    if final_g is not None:
        in_specs.append(_full((1, D_MODEL)))
        args.append(final_g)
    return pl.pallas_call(
        functools.partial(_out_mlp_kernel, len(mixes), perm is not None, final_g is not None),
        grid=(grid,), in_specs=in_specs, out_specs=y_spec,
        out_shape=jax.ShapeDtypeStruct(y.shape, F32),
        compiler_params=_cparams(1), name="out_mlp",
    )(*args)


def _time_major_perm():
    r = LRU_SEQS * LRU_TSTEPS
    p = np.zeros((r, r), np.float32)
    for s in range(LRU_SEQS):
        for t in range(LRU_TSTEPS):
            p[t * LRU_SEQS + s, s * LRU_TSTEPS + t] = 1.0
    return p


def _odd_in_kernel(x_ref, mod_ref, g_ref, w_ref, perm_ref, gate_ref, rec_ref):
    h = _norm_modulate(x_ref[...], g_ref[...], mod_ref[:, 0:1, :], mod_ref[:, 1:2, :])
    h = h.astype(BF16).reshape(-1, D_MODEL)
    h = jnp.dot(perm_ref[...], h, preferred_element_type=F32).astype(BF16)
    gate = jnp.dot(h, w_ref[:, 0:LRU_WIDTH], preferred_element_type=F32)
    gate_ref[...] = _gelu_tanh(gate).astype(BF16)
    rec_ref[...] = jnp.dot(h, w_ref[:, LRU_WIDTH:2 * LRU_WIDTH], preferred_element_type=F32).astype(BF16)


def _odd_in(x3, x_spec, grid, tiles_per_seq, mod, mod_spec, g, w_in, perm):
    n_batch, seq_len, _ = x3.shape
    n_group = n_batch // LRU_SEQS
    rows = LRU_SEQS * LRU_TSTEPS
    ospec = pl.BlockSpec((None, rows, LRU_WIDTH), lambda i: (i // tiles_per_seq, i % tiles_per_seq, 0))
    oshape = jax.ShapeDtypeStruct((n_group, seq_len * LRU_SEQS, LRU_WIDTH), BF16)
    return pl.pallas_call(
        _odd_in_kernel,
        grid=(grid,),
        in_specs=[x_spec, mod_spec, _full((1, D_MODEL)), _full(w_in.shape), _full(perm.shape)],
        out_specs=[ospec, ospec], out_shape=[oshape, oshape],
        compiler_params=_cparams(1), name="odd_in",
    )(x3, mod, g, w_in, perm)


def _lru_kernel(reverse, *refs):
    if reverse:
        (xc_in_ref, gate_ref, hf_ref, wbd_ref, ba_ref, bx_ref, lam_ref, h0_ref,
         out_ref, state_ref, a_ref, b_ref, hout_ref, h_ref) = refs
    else:
        (rec_ref, prev_ref, next_ref, cw_ref, cb_ref, wbd_ref, ba_ref, bx_ref, lam_ref, h0_ref,
         out_ref, xc_out_ref, state_ref, xs_ref, xc_ref, a_ref, b_ref, h_ref) = refs
        hout_ref = out_ref
    rows = a_ref.shape[0]
    c = pl.program_id(1)
    n_chunks = pl.num_programs(1)

    @pl.when(c == 0)
    def _():
        h_ref[...] = h0_ref[...]

    if not reverse:
        xs_ref[SUBLANES:SUBLANES + rows, :] = rec_ref[...].astype(F32)
        xs_ref[0:SUBLANES, :] = jnp.where(c > 0, prev_ref[LRU_HALO - SUBLANES:LRU_HALO, :].astype(F32), 0.0)
        xs_ref[SUBLANES + rows:SUBLANES + rows + LRU_HALO, :] = jnp.where(
            c < n_chunks - 1, next_ref[...].astype(F32), 0.0)
        xc = cb_ref[...] + cw_ref[0:1, :] * xs_ref[4:4 + rows, :]
        for k in range(1, LRU_CONV_K):
            xc = xc + cw_ref[k:k + 1, :] * xs_ref[4 + LRU_SEQS * k:4 + LRU_SEQS * k + rows, :]
        xc_ref[...] = xc
        xc_out_ref[...] = xc.astype(BF16)

    pair = 2 * LRU_BW
    lam = lam_ref[...]
    softplus_neg_lam = jnp.maximum(-lam, 0.0) + jnp.log1p(jnp.exp(-jnp.abs(lam)))
    half_rate = (0.25 * LRU_C) * softplus_neg_lam
    half_ba = 0.5 * ba_ref[...]
    half_bx = 0.5 * bx_ref[...]
    for j in range(LRU_BLOCKS // 2):
        sl = slice(j * pair, (j + 1) * pair)
        if reverse:
            xb = xc_in_ref[:, sl]
            x = xb.astype(F32)
        else:
            x = xc_ref[:, sl]
            xb = x.astype(BF16)
        pre = jnp.dot(xb, wbd_ref[j], preferred_element_type=F32)
        t_r = jnp.tanh(pre[:, 0:pair] + half_ba[:, sl])
        t_i = jnp.tanh(pre[:, pair:2 * pair] + half_bx[:, sl])
        w = jnp.tanh(half_rate[:, sl] + half_rate[:, sl] * t_r)
        inv = 1.0 / (1.0 + w)
        a_ref[:, sl] = (1.0 - w) * inv
        root_w = jnp.where(w > 0.0, w * lax.rsqrt(w), 0.0)
        b_ref[:, sl] = (inv * root_w) * (x + x * t_i)

    n_tiles = rows // SUBLANES
    low = lax.broadcasted_iota(jnp.int32, (SUBLANES, LRU_WIDTH), 0) < LRU_SEQS

    def step(t, h):
        tile = (n_tiles - 1 - t) if reverse else t
        r0 = pl.multiple_of(tile * SUBLANES, SUBLANES)
        a = a_ref[pl.ds(r0, SUBLANES), :]
        b = b_ref[pl.ds(r0, SUBLANES), :]
        h1 = a * h + b
        h2 = a * pltpu.roll(h1, LRU_SEQS, axis=0) + b
        first, second = (h2, h1) if reverse else (h1, h2)
        hout_ref[pl.ds(r0, SUBLANES), :] = jnp.where(low, first, second)
        return pltpu.roll(h2, LRU_SEQS, axis=0)

    h_fin = lax.fori_loop(0, n_tiles, step, h_ref[...], unroll=2)
    h_ref[...] = h_fin
    state_ref[...] = h_fin
    if reverse:
        out_ref[...] = ((hf_ref[...] + hout_ref[...]) * gate_ref[...].astype(F32)).astype(BF16)


def _gate_weights(wa, wx):
    z = jnp.zeros((LRU_BW, LRU_BW), wa.dtype)
    mats = []
    for j in range(LRU_BLOCKS // 2):
        top = jnp.concatenate([wa[2 * j], z, wx[2 * j], z], axis=1)
        bot = jnp.concatenate([z, wa[2 * j + 1], z, wx[2 * j + 1]], axis=1)
        mats.append(jnp.concatenate([top, bot], axis=0))
    return (0.5 * jnp.stack(mats)).astype(BF16)


def _lru_forward(rec, h0, cw, cb, wbd, ba, bx, lam):
    n_group, n_rows, _ = rec.shape
    rows = LRU_ROWS
    nc = n_rows // rows
    halo_blocks = n_rows // LRU_HALO
    per = rows // LRU_HALO
    main = pl.BlockSpec((None, rows, LRU_WIDTH), lambda g, c: (g, c, 0))
    prev = pl.BlockSpec((None, LRU_HALO, LRU_WIDTH), lambda g, c: (g, jnp.maximum(c * per - 1, 0), 0))
    nxt = pl.BlockSpec((None, LRU_HALO, LRU_WIDTH),
                       lambda g, c: (g, jnp.minimum((c + 1) * per, halo_blocks - 1), 0))
    state_spec = pl.BlockSpec((None, SUBLANES, LRU_WIDTH), lambda g, c: (g, 0, 0))
    return pl.pallas_call(
        functools.partial(_lru_kernel, False),
        grid=(n_group, nc),
        in_specs=[main, prev, nxt, _full(cw.shape), _full(cb.shape), _full(wbd.shape), _full(ba.shape),
                  _full(bx.shape), _full(lam.shape), state_spec],
        out_specs=[main, main, state_spec],
        out_shape=[jax.ShapeDtypeStruct(rec.shape, F32), jax.ShapeDtypeStruct(rec.shape, BF16),
                   jax.ShapeDtypeStruct((n_group, SUBLANES, LRU_WIDTH), F32)],
        scratch_shapes=[pltpu.VMEM((rows + SUBLANES + LRU_HALO, LRU_WIDTH), F32),
                        pltpu.VMEM((rows, LRU_WIDTH), F32),
                        pltpu.VMEM((rows, LRU_WIDTH), F32),
                        pltpu.VMEM((rows, LRU_WIDTH), F32),
                        pltpu.VMEM((SUBLANES, LRU_WIDTH), F32)],
        compiler_params=_cparams(2), name="lru_fwd",
    )(rec, rec, rec, cw, cb, wbd, ba, bx, lam, h0)


def _lru_backward(xc, gate, hf, h0, wbd, ba, bx, lam):
    n_group, n_rows, _ = xc.shape
    rows = LRU_ROWS
    nc = n_rows // rows
    main = pl.BlockSpec((None, rows, LRU_WIDTH), lambda g, c: (g, nc - 1 - c, 0))
    state_spec = pl.BlockSpec((None, SUBLANES, LRU_WIDTH), lambda g, c: (g, 0, 0))
    return pl.pallas_call(
        functools.partial(_lru_kernel, True),
        grid=(n_group, nc),
        in_specs=[main, main, main, _full(wbd.shape), _full(ba.shape), _full(bx.shape), _full(lam.shape),
                  state_spec],
        out_specs=[main, state_spec],
        out_shape=[jax.ShapeDtypeStruct(xc.shape, BF16), jax.ShapeDtypeStruct((n_group, SUBLANES, LRU_WIDTH), F32)],
        scratch_shapes=[pltpu.VMEM((rows, LRU_WIDTH), F32),
                        pltpu.VMEM((rows, LRU_WIDTH), F32),
                        pltpu.VMEM((rows, LRU_WIDTH), F32),
                        pltpu.VMEM((SUBLANES, LRU_WIDTH), F32)],
        compiler_params=_cparams(2), name="lru_bwd",
    )(xc, gate, hf, wbd, ba, bx, lam, h0)


def _lru_core(rec, gate, h0_f, h0_b, cw, cb, wa, ba, wx, bx, lam):
    zeros = jnp.zeros_like(h0_f)
    row = lambda v: v.reshape(1, LRU_WIDTH)
    hf, xc, st_f = _lru_forward(rec, jnp.concatenate([h0_f, zeros], axis=1), cw, row(cb),
                                _gate_weights(wa[0], wx[0]), row(ba[0]), row(bx[0]), row(lam[0]))
    y, st_b = _lru_backward(xc, gate, hf, jnp.concatenate([zeros, h0_b], axis=1),
                            _gate_weights(wa[1], wx[1]), row(ba[1]), row(bx[1]), row(lam[1]))
    return y, st_f[:, :LRU_SEQS], st_b[:, LRU_SEQS:]


def kernel(x_prompt, x_sample, c, cache_k, cache_v, state_lru, c_ctx, w_mod, b_mod, norm_mix, norm_ffn, w_ff1, w_ff2, att_in, att_out, att_sink, conv_w, conv_b, conv_norm_g, conv_norm_b, lru_in, lru_out, lru_conv_w, lru_conv_b, lru_wa, lru_ba, lru_wx, lru_bx, lru_lam, final_norm):
    nb_p, t_p, _ = x_prompt.shape
    nb_s, t_s, _ = x_sample.shape
    depth = w_mod.shape[0]
    assert nb_s == LRU_SEQS and nb_p % LRU_SEQS == 0 and nb_s + 1 <= MOD_ROWS
    assert t_p % ROW_TILE == 0 and t_s % ROW_TILE == 0
    ctx_row = nb_s

    cond = jnp.concatenate([c, c_ctx[None, :], jnp.zeros((MOD_ROWS - 1 - nb_s, D_MODEL), F32)], axis=0)
    mod = _modulation(cond, w_mod, b_mod)

    yp = x_prompt.reshape(nb_p * t_p, D_MODEL)
    ys = x_sample.reshape(nb_s * t_s, D_MODEL)
    bf = lambda w: w.astype(BF16)
    vec = lambda v: v.reshape(1, -1)
    perm = jnp.asarray(_time_major_perm(), BF16)

    new_k, new_v, new_h = [], [], []
    for li in range(depth):
        last = li == depth - 1
        final_g = vec(final_norm) if last else None
        w1, w2 = bf(w_ff1[li]), bf(w_ff2[li])
        if li % 2 == 0:
            e = li // 2
            w_in, w_out = bf(att_in[e]), bf(att_out[e])
            sink_tab = jnp.broadcast_to(att_sink[e][:, None], (N_HEADS, LANES))
            cpar = (conv_w[e], vec(conv_b[e]), vec(conv_norm_g[e]), vec(conv_norm_b[e]))
            w_outs = [w_out[:ATT_WIDTH], w_out[ATT_WIDTH:]]
            tm = MLP_ROW_TILE
            y_spec = pl.BlockSpec((tm, D_MODEL), lambda i: (i, 0))
            mix_specs = [pl.BlockSpec((tm, ATT_WIDTH), lambda i: (i, 0)),
                         pl.BlockSpec((tm, CONV_CH), lambda i: (i, 0))]
            q, kb, vt, u, k, v = _even_in(yp, mod, li, lambda i: ctx_row, vec(norm_mix[li]), w_in, nb_p, t_p, None)
            new_k.append(k.reshape(nb_p, t_p, N_KV_HEADS, HEAD_DIM))
            new_v.append(v.reshape(nb_p, t_p, N_KV_HEADS, HEAD_DIM))
            o_att = _attention_ctx(q, kb, vt, sink_tab, nb_p, t_p)
            o_conv = _conformer_conv(u, nb_p, t_p, *cpar)
            yp = _out_mlp(yp, y_spec, yp.shape[0] // tm, [o_att, o_conv], mix_specs, w_outs, mod,
                          _mod_spec(li, lambda i: ctx_row), vec(norm_ffn[li]), w1, w2, None, final_g)
            past = cache_k.shape[2]
            ck, cvt = _kv_prep(cache_k[:, e].reshape(nb_s * past, KV_WIDTH),
                               cache_v[:, e].reshape(nb_s * past, KV_WIDTH), nb_s, past)
            q, kb, vt, u = _even_in(ys, mod, li, lambda i: (i * ROW_TILE) // t_s, vec(norm_mix[li]), w_in,
                                    nb_s, t_s, _rope_tables(t_s))
            o_att = _attention_lat(q, kb, vt, ck, cvt, sink_tab, nb_s, t_s, past)
            o_conv = _conformer_conv(u, nb_s, t_s, *cpar)
            ys = _out_mlp(ys, y_spec, ys.shape[0] // tm, [o_att, o_conv], mix_specs, w_outs, mod,
                          _mod_spec(li, lambda i: (i * tm) // t_s), vec(norm_ffn[li]), w1, w2, None, final_g)
        else:
            o = li // 2
            w_in, w_out = bf(lru_in[o]), bf(lru_out[o])
            lpar = (lru_conv_w[o], lru_conv_b[o], lru_wa[o], lru_ba[o], lru_wx[o], lru_bx[o], lru_lam[o])
            rows = LRU_SEQS * LRU_TSTEPS

            def lru_stream(y, n_batch, seq_len, mod_spec, h0_f, h0_b):
                tiles_per_seq = seq_len // LRU_TSTEPS
                grid = (n_batch // LRU_SEQS) * tiles_per_seq
                y3 = y.reshape(n_batch, seq_len, D_MODEL)
                y_spec = pl.BlockSpec((LRU_SEQS, LRU_TSTEPS, D_MODEL),
                                      lambda i: (i // tiles_per_seq, i % tiles_per_seq, 0))
                gate, rec = _odd_in(y3, y_spec, grid, tiles_per_seq, mod, mod_spec, vec(norm_mix[li]), w_in, perm)
                y_lru, st_f, st_b = _lru_core(rec, gate, h0_f, h0_b, *lpar)
                mix_spec = pl.BlockSpec((None, rows, LRU_WIDTH), lambda i: (i // tiles_per_seq, i % tiles_per_seq, 0))
                y_new = _out_mlp(y3, y_spec, grid, [y_lru], [mix_spec], [w_out], mod, mod_spec,
                                 vec(norm_ffn[li]), w1, w2, perm.T, final_g)
                return y_new.reshape(n_batch * seq_len, D_MODEL), st_f, st_b

            zeros_p = jnp.zeros((nb_p // LRU_SEQS, LRU_SEQS, LRU_WIDTH), F32)
            yp, st_f, st_b = lru_stream(yp, nb_p, t_p, _mod_group_spec(li, 1, ctx_row), zeros_p, zeros_p)
            new_h.append(jnp.stack([st_f.reshape(nb_p, LRU_WIDTH), st_b.reshape(nb_p, LRU_WIDTH)], axis=1))
            ys, _, _ = lru_stream(ys, nb_s, t_s, _mod_group_spec(li, LRU_SEQS, 0),
                                  state_lru[:, o, 0][None].astype(F32), state_lru[:, o, 1][None].astype(F32))

    y_prompt = yp.reshape(nb_p, t_p, D_MODEL)
    y_sample = ys.reshape(nb_s, t_s, D_MODEL)
    return (y_prompt, y_sample, jnp.stack(new_k, axis=1), jnp.stack(new_v, axis=1), jnp.stack(new_h, axis=1))
```
